```python
import functools
import jax, jax.numpy as jnp
from jax import lax
import numpy as np

D_MODEL = 1024
BATCH = 8
SEQ = 8192
DEPTH = 2
DEC_BATCH = 8
DEC_SEQ = 16
PAST_LEN = 2048

CHUNK = 64
HEAD_DIM = 64
MIX_HEADS = (3 * D_MODEL // 4) // HEAD_DIM
MIX_WIDTH = MIX_HEADS * HEAD_DIM
MEM_HEADS = 4
MEM_WIDTH = MEM_HEADS * HEAD_DIM
N_MEM = 256
DECAY_LORA = 64
AAA_LORA = 64
GATE_LORA = 128
GN_EPS = 64e-5
A_COLS = 3 * MIX_WIDTH + DECAY_LORA + AAA_LORA + GATE_LORA
A_SPLITS = (MIX_WIDTH, 2 * MIX_WIDTH, 3 * MIX_WIDTH, 3 * MIX_WIDTH + DECAY_LORA,
            3 * MIX_WIDTH + DECAY_LORA + AAA_LORA)
WINDOW = 128
WINDOW_CHUNKS = WINDOW // CHUNK
N_KV_HEADS = 2
GQA_GROUP = MIX_HEADS // N_KV_HEADS
ATT_SCALE = HEAD_DIM ** -0.5
D_FF = 7 * D_MODEL // 2
N_EXPERTS = 8
TOP_K = 2
FFN_BLOCK = 128
N_A_LAYERS = DEPTH // 2
N_B_LAYERS = DEPTH - N_A_LAYERS
N_DENSE = (DEPTH + 1) // 2
N_MOE = DEPTH // 2
DN_ALPHA = (2 * DEPTH) ** 0.25
DN_BETA = (8 * DEPTH) ** -0.25
LN_EPS = 1e-5

kernel_name = "rwkv7_swa_sink_yoco_stream_step"


def layer_norm(x, g, b):
    xf = x.astype(jnp.float32)
    mu = jnp.mean(xf, -1, keepdims=True)
    var = jnp.mean(jnp.square(xf - mu), -1, keepdims=True)
    return ((xf - mu) * lax.rsqrt(var + LN_EPS) * g + b).astype(x.dtype)


def wkv_scan(s0, r, decay, k, v, kk, a):
    def step(s, inp):
        r_t, w_t, k_t, v_t, kk_t, a_t = inp
        sa = jnp.einsum('bhvk,bhk->bhv', s, -kk_t)
        s = (s * w_t[:, :, None, :] + sa[..., None] * (kk_t * a_t)[:, :, None, :]
             + v_t[..., None] * k_t[:, :, None, :])
        return s, jnp.einsum('bhvk,bhk->bhv', s, r_t)
    xs = tuple(jnp.moveaxis(t, 1, 0) for t in (r, decay, k, v, kk, a))
    s_fin, ys = lax.scan(step, s0, xs)
    return s_fin, jnp.moveaxis(ys, 0, 1)


def rwkv7_time_mix(x, x_prev, s0, w_in, mu, w0, w2, a0, a2, g2, k_k, k_a, r_k, gn_g, gn_b):
    B, T, _ = x.shape
    proj = x @ w_in
    pa, q_mem = proj[..., :A_COLS], proj[..., A_COLS:]
    pa_prev = jnp.concatenate([x_prev @ w_in[:, :A_COLS], pa[:, :-1]], axis=1)
    z = pa + (pa_prev - pa) * mu
    r, k, v, zw, za, zg = jnp.split(z, A_SPLITS, axis=-1)
    w_log = -jax.nn.softplus(-(w0 + jnp.tanh(zw) @ w2)) - 0.5
    decay = jnp.exp(-jnp.exp(w_log.astype(jnp.float32)))
    a = jax.nn.sigmoid(a0 + za @ a2)
    g = jax.nn.sigmoid(zg) @ g2
    heads = lambda t: t.reshape(B, T, MIX_HEADS, HEAD_DIM).astype(jnp.float32)
    kk = heads(k * k_k)
    kk = kk * lax.rsqrt(jnp.maximum(jnp.sum(kk * kk, -1, keepdims=True), 1e-24))
    k = k * (1.0 + (a - 1.0) * k_a)
    rh, kh, vh = heads(r), heads(k), heads(v)
    s_fin, y = wkv_scan(s0.astype(jnp.float32), rh, heads(decay), kh, vh, kk, heads(a))
    ym = jnp.mean(y, -1, keepdims=True)
    yv = jnp.mean(jnp.square(y - ym), -1, keepdims=True)
    yn = ((y - ym) * lax.rsqrt(yv + GN_EPS)).reshape(B, T, MIX_WIDTH) * gn_g + gn_b
    bonus = (jnp.sum(rh * kh * r_k, -1, keepdims=True) * vh).reshape(B, T, MIX_WIDTH)
    out = ((yn + bonus) * g).astype(x.dtype)
    return out, q_mem, s_fin


def shared_kv(h, w_kv):
    B, T, _ = h.shape
    kv = (h @ w_kv).reshape(B, T, 2, N_KV_HEADS, HEAD_DIM)
    return kv[:, :, 0], kv[:, :, 1]


def sink_attend(s, v, sinks, eq):
    sk = sinks.astype(jnp.float32).reshape(N_KV_HEADS, GQA_GROUP, 1, 1)
    m = jnp.maximum(jnp.max(s, -1, keepdims=True), sk)
    p = jnp.exp(s - m)
    p = p / (jnp.sum(p, -1, keepdims=True) + jnp.exp(sk - m))
    return jnp.einsum(eq, p.astype(v.dtype), v)


def swa_prompt(q, k, v, sinks):
    B, S = q.shape[:2]
    nc = S // CHUNK
    qc = q.reshape(B, nc, CHUNK, N_KV_HEADS, GQA_GROUP, HEAD_DIM)

    def band(t):
        tc = t.reshape(B, nc, CHUNK, N_KV_HEADS, HEAD_DIM)
        prev = [jnp.pad(tc, ((0, 0), (j, 0), (0, 0), (0, 0), (0, 0)))[:, :nc]
                for j in range(WINDOW_CHUNKS, 0, -1)]
        return jnp.concatenate(prev + [tc], axis=2)

    kb, vb = band(k), band(v)
    offs = WINDOW_CHUNKS - jnp.arange((WINDOW_CHUNKS + 1) * CHUNK) // CHUNK
    valid = (jnp.arange(nc)[:, None] - offs[None, :]) >= 0
    s = jnp.einsum('bnqkgd,bnlkd->bnkgql', qc, kb).astype(jnp.float32) * ATT_SCALE
    s = jnp.where(valid[None, :, None, None, None, :], s, -jnp.inf)
    o = sink_attend(s, vb, sinks, 'bnkgql,bnlkd->bnqkgd')
    return o.reshape(B, S, MIX_WIDTH)


def swa_sample(q, k_cache, v_cache, k_new, v_new, sinks):
    B, T = q.shape[:2]
    qg = q.reshape(B, T, N_KV_HEADS, GQA_GROUP, HEAD_DIM)
    kb = jnp.concatenate([k_cache.astype(k_new.dtype), k_new], axis=1)
    vb = jnp.concatenate([v_cache.astype(v_new.dtype), v_new], axis=1)
    s = jnp.einsum('btkgd,blkd->bkgtl', qg, kb).astype(jnp.float32) * ATT_SCALE
    o = sink_attend(s, vb, sinks, 'bkgtl,blkd->btkgd')
    return o.reshape(B, T, MIX_WIDTH)


def mem_attend(q, mk, mv):
    B, T, _ = q.shape
    qh = q.reshape(B, T, MEM_HEADS, HEAD_DIM)
    s = jnp.einsum('bthd,bmhd->bhtm', qh, mk.astype(q.dtype)).astype(jnp.float32) * ATT_SCALE
    p = jax.nn.softmax(s, axis=-1).astype(q.dtype)
    return jnp.einsum('bhtm,bmhd->bthd', p, mv.astype(q.dtype)).reshape(B, T, MEM_WIDTH)


def swiglu(x, w1, w2):
    gate, up = jnp.split(x @ w1, 2, axis=-1)
    return (jax.nn.silu(gate) * up) @ w2


def moe_swiglu(x, router, w1, w2):
    logits = (x @ router).astype(jnp.float32)
    top_val, top_idx = lax.top_k(logits, TOP_K)
    gates = jax.nn.softmax(top_val, axis=-1)
    gate_full = jnp.einsum('nk,nke->ne', gates, jax.nn.one_hot(top_idx, N_EXPERTS, dtype=jnp.float32))
    gate, up = jnp.split(jnp.einsum('nd,edf->nef', x, w1), 2, axis=-1)
    act = jax.nn.silu(gate) * up * gate_full[:, :, None].astype(x.dtype)
    return jnp.einsum('nef,efd->nd', act, w2)


def token_blocks(fn, x):
    flat = x.reshape(-1, FFN_BLOCK, x.shape[-1])
    return lax.map(fn, flat).reshape(x.shape)


def setup_inputs(seed: int = 0) -> dict:
    key = jax.random.key(seed)
    ks = iter(jax.random.split(key, 48))
    nrm = lambda shape, scale: jax.random.normal(next(ks), shape, jnp.float32) * scale
    unif = lambda shape, lo, hi: jax.random.uniform(next(ks), shape, jnp.float32, lo, hi)
    swa_rows = min(WINDOW, PAST_LEN)
    return {
        "x_prompt": nrm((BATCH, SEQ, D_MODEL), 1.0),
        "mem_prompt": nrm((BATCH, N_MEM, D_MODEL), 1.0),
        "x_sample": nrm((DEC_BATCH, DEC_SEQ, D_MODEL), 1.0),
        "cache_swa_k": nrm((DEC_BATCH, swa_rows, N_KV_HEADS, HEAD_DIM), 1.0),
        "cache_swa_v": nrm((DEC_BATCH, swa_rows, N_KV_HEADS, HEAD_DIM), 1.0),
        "cache_mem_k": nrm((DEPTH, DEC_BATCH, N_MEM, MEM_HEADS, HEAD_DIM), 1.0),
        "cache_mem_v": nrm((DEPTH, DEC_BATCH, N_MEM, MEM_HEADS, HEAD_DIM), 1.0),
        "state_wkv": nrm((N_A_LAYERS, DEC_BATCH, MIX_HEADS, HEAD_DIM, HEAD_DIM), 0.5),
        "state_shift": nrm((N_A_LAYERS, DEC_BATCH, D_MODEL), 1.0),
        "w_in_a": nrm((N_A_LAYERS, D_MODEL, A_COLS + MEM_WIDTH), D_MODEL ** -0.5),
        "mu_a": unif((N_A_LAYERS, A_COLS), 0.0, 1.0),
        "w0": unif((N_A_LAYERS, MIX_WIDTH), -6.0, -1.0),
        "w2_decay": nrm((N_A_LAYERS, DECAY_LORA, MIX_WIDTH), 0.5 * DECAY_LORA ** -0.5),
        "a0": nrm((N_A_LAYERS, MIX_WIDTH), 0.1),
        "a2_iclr": nrm((N_A_LAYERS, AAA_LORA, MIX_WIDTH), 0.5 * AAA_LORA ** -0.5),
        "g2_gate": nrm((N_A_LAYERS, GATE_LORA, MIX_WIDTH), GATE_LORA ** -0.5),
        "k_k": 0.85 + nrm((N_A_LAYERS, MIX_WIDTH), 0.05),
        "k_a": 1.0 + nrm((N_A_LAYERS, MIX_WIDTH), 0.05),
        "r_k": nrm((N_A_LAYERS, MIX_HEADS, HEAD_DIM), 0.1),
        "gn_g": 1.0 + nrm((N_A_LAYERS, MIX_WIDTH), 0.05),
        "gn_b": nrm((N_A_LAYERS, MIX_WIDTH), 0.02),
        "w_in_b": nrm((N_B_LAYERS, D_MODEL, MIX_WIDTH + MEM_WIDTH), D_MODEL ** -0.5),
        "sinks": nrm((N_B_LAYERS, MIX_HEADS), 1.0),
        "w_kv_shared": nrm((D_MODEL, 2 * N_KV_HEADS * HEAD_DIM), D_MODEL ** -0.5),
        "w_mem_kv": nrm((DEPTH, D_MODEL, 2 * MEM_WIDTH), D_MODEL ** -0.5),
        "w_out": nrm((DEPTH, MIX_WIDTH + MEM_WIDTH, D_MODEL), DN_BETA * (MIX_WIDTH + MEM_WIDTH) ** -0.5),
        "ln1_g": 1.0 + nrm((DEPTH, D_MODEL), 0.05),
        "ln1_b": nrm((DEPTH, D_MODEL), 0.02),
        "ln2_g": 1.0 + nrm((DEPTH, D_MODEL), 0.05),
        "ln2_b": nrm((DEPTH, D_MODEL), 0.02),
        "ffn_w1": nrm((N_DENSE, D_MODEL, 2 * D_FF), D_MODEL ** -0.5),
        "ffn_w2": nrm((N_DENSE, D_FF, D_MODEL), DN_BETA * D_FF ** -0.5),
        "router": nrm((N_MOE, D_MODEL, N_EXPERTS), D_MODEL ** -0.5),
        "moe_w1": nrm((N_MOE, N_EXPERTS, D_MODEL, 2 * D_FF), D_MODEL ** -0.5),
        "moe_w2": nrm((N_MOE, N_EXPERTS, D_FF, D_MODEL), DN_BETA * D_FF ** -0.5),
    }


def reference(x_prompt, mem_prompt, x_sample, cache_swa_k, cache_swa_v, cache_mem_k, cache_mem_v,
              state_wkv, state_shift, w_in_a, mu_a, w0, w2_decay, a0, a2_iclr, g2_gate, k_k, k_a,
              r_k, gn_g, gn_b, w_in_b, sinks, w_kv_shared, w_mem_kv, w_out, ln1_g, ln1_b, ln2_g,
              ln2_b, ffn_w1, ffn_w2, router, moe_w1, moe_w2):
    xp, xs = x_prompt, x_sample
    bp, sp, _ = xp.shape
    bs, ts, _ = xs.shape
    wkv_p, wkv_s, shift_p, shift_s, memk_p, memv_p = [], [], [], [], [], []
    for layer in range(DEPTH):
        mkv = (mem_prompt @ w_mem_kv[layer]).reshape(bp, -1, 2, MEM_HEADS, HEAD_DIM)
        mk_p, mv_p = mkv[:, :, 0], mkv[:, :, 1]
        memk_p.append(mk_p)
        memv_p.append(mv_p)
        if layer < N_A_LAYERS:
            j = layer
            prm = (w_in_a[j], mu_a[j], w0[j], w2_decay[j], a0[j], a2_iclr[j], g2_gate[j],
                   k_k[j], k_a[j], r_k[j], gn_g[j], gn_b[j])
            mix_p, qm_p, st_p = rwkv7_time_mix(
                xp, jnp.zeros_like(xp[:, :1]),
                jnp.zeros((bp, MIX_HEADS, HEAD_DIM, HEAD_DIM), jnp.float32), *prm)
            mix_s, qm_s, st_s = rwkv7_time_mix(
                xs, state_shift[j][:, None].astype(xs.dtype), state_wkv[j], *prm)
            wkv_p.append(st_p.astype(xp.dtype))
            wkv_s.append(st_s.astype(xs.dtype))
            shift_p.append(xp[:, -1])
            shift_s.append(xs[:, -1])
        else:
            if layer == N_A_LAYERS:
                k_p, v_p = shared_kv(xp, w_kv_shared)
                k_s, v_s = shared_kv(xs, w_kv_shared)
            j = layer - N_A_LAYERS
            hp = xp @ w_in_b[j]
            hs = xs @ w_in_b[j]
            mix_p = swa_prompt(hp[..., :MIX_WIDTH].reshape(bp, sp, MIX_HEADS, HEAD_DIM), k_p, v_p, sinks[j])
            mix_s = swa_sample(hs[..., :MIX_WIDTH].reshape(bs, ts, MIX_HEADS, HEAD_DIM),
                               cache_swa_k, cache_swa_v, k_s, v_s, sinks[j])
            qm_p = hp[..., MIX_WIDTH:]
            qm_s = hs[..., MIX_WIDTH:]
        mem_p = mem_attend(qm_p, mk_p, mv_p)
        mem_s = mem_attend(qm_s, cache_mem_k[layer], cache_mem_v[layer])
        xp = layer_norm(DN_ALPHA * xp + jnp.concatenate([mix_p, mem_p], -1) @ w_out[layer],
                        ln1_g[layer], ln1_b[layer])
        xs = layer_norm(DN_ALPHA * xs + jnp.concatenate([mix_s, mem_s], -1) @ w_out[layer],
                        ln1_g[layer], ln1_b[layer])
        f = layer // 2
        if layer % 2 == 0:
            ffn = functools.partial(swiglu, w1=ffn_w1[f], w2=ffn_w2[f])
        else:
            ffn = functools.partial(moe_swiglu, router=router[f], w1=moe_w1[f], w2=moe_w2[f])
        xp = layer_norm(DN_ALPHA * xp + token_blocks(ffn, xp), ln2_g[layer], ln2_b[layer])
        xs = layer_norm(DN_ALPHA * xs + ffn(xs.reshape(-1, D_MODEL)).reshape(xs.shape),
                        ln2_g[layer], ln2_b[layer])
    y_prompt = xp
    y_sample = xs
    swa_k_prompt = k_p[:, -WINDOW:]
    swa_v_prompt = v_p[:, -WINDOW:]
    swa_k_sample = k_s
    swa_v_sample = v_s
    mem_k_prompt = jnp.stack(memk_p)
    mem_v_prompt = jnp.stack(memv_p)
    wkv_prompt = jnp.stack(wkv_p)
    shift_prompt = jnp.stack(shift_p)
    wkv_sample = jnp.stack(wkv_s)
    shift_sample = jnp.stack(shift_s)
    return (y_prompt, y_sample, swa_k_prompt, swa_v_prompt, swa_k_sample, swa_v_sample,
            mem_k_prompt, mem_v_prompt, wkv_prompt, shift_prompt, wkv_sample, shift_sample)
```

```python
import functools

import jax
import jax.numpy as jnp
from jax import lax
from jax.experimental import pallas as pl
from jax.experimental.pallas import tpu as pltpu

F32 = jnp.float32
BF16 = jnp.bfloat16

HEAD_DIM = 64
LANES = 128
MIX_HEADS = 12
MIX_WIDTH = MIX_HEADS * HEAD_DIM
N_PAIRS = MIX_HEADS // 2
MEM_HEADS = 4
MEM_WIDTH = MEM_HEADS * HEAD_DIM
DECAY_LORA = 64
AAA_LORA = 64
GATE_LORA = 128
A_COLS = 3 * MIX_WIDTH + DECAY_LORA + AAA_LORA + GATE_LORA
GN_EPS = 64e-5
WINDOW = 128
N_KV_HEADS = 2
GQA_GROUP = MIX_HEADS // N_KV_HEADS
ATT_SCALE = HEAD_DIM ** -0.5
N_EXPERTS = 8
DEPTH = 2
DN_ALPHA = (2 * DEPTH) ** 0.25
LN_EPS = 1e-5
WKV_CHUNK = 64
VMEM_LIMIT = 56 * 1024 * 1024


def _params(*sem):
    return pltpu.CompilerParams(dimension_semantics=sem, vmem_limit_bytes=VMEM_LIMIT)


def _dot(a, b, hp=False):
    if hp:
        return jnp.dot(a.astype(F32), b.astype(F32), preferred_element_type=F32,
                       precision=lax.Precision.HIGHEST)
    return jnp.dot(a.astype(BF16), b.astype(BF16), preferred_element_type=F32)


def _dot_nt(a, b, hp=False):
    dims = (((1,), (1,)), ((), ()))
    if hp:
        return lax.dot_general(a.astype(F32), b.astype(F32), dims, preferred_element_type=F32,
                               precision=lax.Precision.HIGHEST)
    return lax.dot_general(a.astype(BF16), b.astype(BF16), dims, preferred_element_type=F32)


def _split3(a):
    hi = a.astype(BF16)
    r1 = a - hi.astype(F32)
    mid = r1.astype(BF16)
    lo = (r1 - mid.astype(F32)).astype(BF16)
    return hi, mid, lo


def _sigmoid(x):
    return 1.0 / (1.0 + jnp.exp(-x))


def _layer_norm(v, g, b):
    mu = jnp.mean(v, -1, keepdims=True)
    d = v - mu
    var = jnp.mean(d * d, -1, keepdims=True)
    return d * lax.rsqrt(var + LN_EPS) * g + b


def _mm_kernel(x_ref, w_ref, o_ref, *, hp):
    o_ref[...] = _dot(x_ref[...], w_ref[...], hp)


def _mm(x, w, tm, tn=None, hp=False):
    m, k = x.shape
    n = w.shape[1]
    tm = min(tm, m)
    tn = n if tn is None else tn
    return pl.pallas_call(
        functools.partial(_mm_kernel, hp=hp),
        grid=(m // tm, n // tn),
        in_specs=[pl.BlockSpec((tm, k), lambda i, j: (i, 0)),
                  pl.BlockSpec((k, tn), lambda i, j: (0, j))],
        out_specs=pl.BlockSpec((tm, tn), lambda i, j: (i, j)),
        out_shape=jax.ShapeDtypeStruct((m, n), F32),
        compiler_params=_params("arbitrary", "arbitrary"),
        name="proj_mm",
    )(x, w)


def _stack_masked(x, lane_lo):
    return jnp.concatenate([jnp.where(lane_lo, x, 0.0), jnp.where(lane_lo, 0.0, x)], axis=0)


def _rwkv_kernel(proj_ref, prev0_ref, s0_ref, mu_ref, w0_ref, wwa_ref, a0_ref, g2_ref,
                 kk_ref, ka_ref, rk_ref, gng_ref, gnb_ref, hsum_ref,
                 mix_ref, sfin_ref, s_ref, prev_ref, *, t_valid, hp):
    dot = functools.partial(_dot, hp=hp)
    dot_nt = functools.partial(_dot_nt, hp=hp)
    c = pl.program_id(1)
    n_c = pl.num_programs(1)
    C = proj_ref.shape[1]

    @pl.when(c == 0)
    def _():
        s_ref[...] = s0_ref[0]
        prev_ref[...] = prev0_ref[0]

    pa = proj_ref[0][:, :A_COLS]
    row = lax.broadcasted_iota(jnp.int32, (C, 1), 0)
    pa_prev = jnp.where(row == 0, prev_ref[...], pltpu.roll(pa, 1, 0))
    prev_ref[...] = pa[C - 1:C, :]
    z = pa + (pa_prev - pa) * mu_ref[...]
    r = z[:, :MIX_WIDTH]
    k = z[:, MIX_WIDTH:2 * MIX_WIDTH]
    v = z[:, 2 * MIX_WIDTH:3 * MIX_WIDTH]
    zwa = z[:, 3 * MIX_WIDTH:3 * MIX_WIDTH + LANES]
    zg = z[:, 3 * MIX_WIDTH + LANES:A_COLS]
    lane = lax.broadcasted_iota(jnp.int32, (C, LANES), 1)
    lane_lo = lane < HEAD_DIM
    wa = dot(jnp.where(lane_lo, jnp.tanh(zwa), zwa), wwa_ref[...])
    wl = w0_ref[...] + wa[:, :MIX_WIDTH]
    softplus = jnp.maximum(-wl, 0.0) + jnp.log(1.0 + jnp.exp(-jnp.abs(wl)))
    lw = -jnp.exp(-softplus - 0.5)
    a = _sigmoid(a0_ref[...] + wa[:, MIX_WIDTH:])
    g = dot(_sigmoid(zg), g2_ref[...])
    kk = k * kk_ref[...]
    k2 = k * (1.0 + (a - 1.0) * ka_ref[...])
    rk = r * k2 * rk_ref[...]
    hsum = hsum_ref[...]
    sums = _dot_x3_rhs_exact(jnp.concatenate([kk * kk, rk], axis=0), hsum)
    kk = kk * lax.rsqrt(jnp.maximum(sums[:C], 1e-24))
    bonus = sums[C:] * v
    if t_valid < C:
        live = row < t_valid
        lw = jnp.where(live, lw, 0.0)
        kk = jnp.where(live, kk, 0.0)
        k2 = jnp.where(live, k2, 0.0)

    ri = lax.broadcasted_iota(jnp.int32, (C, C), 0)
    ci = lax.broadcasted_iota(jnp.int32, (C, C), 1)
    tri = jnp.where(ri >= ci, 1.0, 0.0).astype(BF16)
    h3, m3, l3 = _split3(lw)
    d = functools.partial(jnp.dot, preferred_element_type=F32)
    cum = d(tri, h3) + (d(tri, m3) + d(tri, l3))
    cum_last = cum[C - 1:C, :]
    e_inc = jnp.exp(cum)
    e_ninc = jnp.exp(-cum)
    kka = kk * a
    at = -kk * jnp.exp(cum - lw)
    rt = r * e_inc
    bt = kka * e_ninc
    kt = k2 * e_ninc
    e_rem = jnp.exp(cum_last - cum)
    bg = kka * e_rem
    kg = k2 * e_rem
    gam = jnp.exp(cum_last)

    C2 = 2 * C
    r2 = lax.broadcasted_iota(jnp.int32, (C2, C2), 0)
    c2 = lax.broadcasted_iota(jnp.int32, (C2, C2), 1)
    strict = r2 > c2
    incl = r2 >= c2
    wc = lax.broadcasted_iota(jnp.int32, (C, C2), 1)
    w_lo = wc < C
    bd_r = lax.broadcasted_iota(jnp.int32, (LANES, LANES), 0) < HEAD_DIM
    bd_c = lax.broadcasted_iota(jnp.int32, (LANES, LANES), 1) < HEAD_DIM
    bd_mask = bd_r == bd_c

    def wide(x):
        return x[:C] + x[C:]

    def blockdiag(x):
        return jnp.concatenate([jnp.where(w_lo, x, 0.0), jnp.where(w_lo, 0.0, x)], axis=0)

    outs = []
    for p in range(N_PAIRS):
        sl = slice(p * LANES, (p + 1) * LANES)
        at_p, rt_p, bt_p, kt_p, v_p = at[:, sl], rt[:, sl], bt[:, sl], kt[:, sl], v[:, sl]
        lhs = jnp.concatenate([_stack_masked(at_p, lane_lo), _stack_masked(rt_p, lane_lo)], axis=0)
        rhs = jnp.concatenate([_stack_masked(bt_p, lane_lo), _stack_masked(kt_p, lane_lo)], axis=0)
        gm = dot_nt(lhs, rhs)
        l_ab = wide(jnp.where(strict, gm[:C2, :C2], 0.0))
        l_ak = wide(jnp.where(strict, gm[:C2, C2:], 0.0))
        a_rb = wide(jnp.where(incl, gm[C2:, :C2], 0.0))
        a_rk = wide(jnp.where(incl, gm[C2:, C2:], 0.0))
        n_inv = l_ab
        pw = l_ab
        span = 1
        while span * 2 < C:
            pw = dot(pw, blockdiag(pw))
            n_inv = n_inv + pw + dot(n_inv, blockdiag(pw))
            span *= 2
        s_p = s_ref[p]
        a_s = dot_nt(jnp.concatenate([at_p, rt_p], axis=0), s_p)
        v_sm = _stack_masked(v_p, lane_lo)
        x = a_s[:C] + dot(l_ak, v_sm)
        u = x + dot(n_inv, _stack_masked(x, lane_lo))
        o = a_s[C:] + dot(jnp.concatenate([a_rb, a_rk], axis=1),
                          jnp.concatenate([_stack_masked(u, lane_lo), v_sm], axis=0))
        uv_t = jnp.concatenate([u, v_p], axis=0).T
        upd = dot(uv_t, jnp.concatenate([bg[:, sl], kg[:, sl]], axis=0))
        s_ref[p] = s_p * gam[:, sl] + jnp.where(bd_mask, upd, 0.0)
        outs.append(o)
    y = jnp.concatenate(outs, axis=1)

    inv_n = 1.0 / HEAD_DIM
    ym = _dot_x3_rhs_exact(y, hsum) * inv_n
    yd = y - ym
    yv = _dot_x3_rhs_exact(yd * yd, hsum) * inv_n
    yn = yd * lax.rsqrt(yv + GN_EPS) * gng_ref[...] + gnb_ref[...]
    mix_ref[0] = (yn + bonus) * g

    @pl.when(c == n_c - 1)
    def _():
        sfin_ref[0] = s_ref[...]


def _dot_x3_rhs_exact(a, b_exact):
    h, m, l = _split3(a)
    d = functools.partial(jnp.dot, preferred_element_type=F32)
    return d(h, b_exact) + (d(m, b_exact) + d(l, b_exact))


def _rwkv(proj, prev0, s0_bd, prm, t_valid, hp):
    bsz, t, npj = proj.shape
    C = WKV_CHUNK
    full = lambda shp: pl.BlockSpec(shp, lambda b, c: (0,) * len(shp))
    in_specs = [
        pl.BlockSpec((1, C, npj), lambda b, c: (b, c, 0)),
        pl.BlockSpec((1, 1, A_COLS), lambda b, c: (b, 0, 0)),
        pl.BlockSpec((1, N_PAIRS, LANES, LANES), lambda b, c: (b, 0, 0, 0)),
    ] + [full(x.shape) for x in prm]
    return pl.pallas_call(
        functools.partial(_rwkv_kernel, t_valid=t_valid, hp=hp),
        grid=(bsz, t // C),
        in_specs=in_specs,
        out_specs=[pl.BlockSpec((1, C, MIX_WIDTH), lambda b, c: (b, c, 0)),
                   pl.BlockSpec((1, N_PAIRS, LANES, LANES), lambda b, c: (b, 0, 0, 0))],
        out_shape=[jax.ShapeDtypeStruct((bsz, t, MIX_WIDTH), F32),
                   jax.ShapeDtypeStruct((bsz, N_PAIRS, LANES, LANES), F32)],
        scratch_shapes=[pltpu.VMEM((N_PAIRS, LANES, LANES), F32),
                        pltpu.VMEM((1, A_COLS), F32)],
        compiler_params=_params("arbitrary", "arbitrary"),
        name="rwkv7_chunked",
    )(proj, prev0, s0_bd, *prm)


def _state_to_blockdiag(s):
    bsz = s.shape[0]
    sp = s.reshape(bsz, N_PAIRS, 2, HEAD_DIM, HEAD_DIM)
    z = jnp.zeros_like(sp[:, :, 0])
    top = jnp.concatenate([sp[:, :, 0], z], axis=-1)
    bot = jnp.concatenate([z, sp[:, :, 1]], axis=-1)
    return jnp.concatenate([top, bot], axis=-2)


def _state_from_blockdiag(s_bd):
    bsz = s_bd.shape[0]
    h0 = s_bd[:, :, :HEAD_DIM, :HEAD_DIM]
    h1 = s_bd[:, :, HEAD_DIM:, HEAD_DIM:]
    return jnp.stack([h0, h1], axis=2).reshape(bsz, MIX_HEADS, HEAD_DIM, HEAD_DIM)


def _attn_out_kernel(mix_ref, qm_ref, mk_ref, mv_ref, x_ref, wmix_ref, wmem_ref, g_ref, b_ref, o_ref,
                     *, hp):
    qm = qm_ref[0]
    mk = mk_ref[0] if hp else mk_ref[0].astype(BF16)
    mv = mv_ref[0] if hp else mv_ref[0].astype(BF16)
    lane = lax.broadcasted_iota(jnp.int32, qm.shape, 1)
    mem = jnp.zeros_like(qm)
    for h in range(MEM_HEADS):
        in_h = (lane >= h * HEAD_DIM) & (lane < (h + 1) * HEAD_DIM)
        s = _dot_nt(jnp.where(in_h, qm, 0.0), mk, hp) * ATT_SCALE
        p = jnp.exp(s - jnp.max(s, -1, keepdims=True))
        den = jnp.sum(p, -1, keepdims=True)
        mem = jnp.where(in_h, _dot(p, mv, hp) / den, mem)
    y = _dot(mix_ref[0], wmix_ref[...], hp) + _dot(mem, wmem_ref[...], hp)
    o_ref[0] = _layer_norm(DN_ALPHA * x_ref[0] + y, g_ref[...], b_ref[...])


def _attn_out(mix, qsrc, q_col_block, mk, mv, x, w_mix, w_mem, g, b, tq, hp):
    bsz, t, d = x.shape
    tq = min(tq, t)
    n_mem = mk.shape[1]
    return pl.pallas_call(
        functools.partial(_attn_out_kernel, hp=hp),
        grid=(bsz, t // tq),
        in_specs=[
            pl.BlockSpec((1, tq, MIX_WIDTH), lambda bb, i: (bb, i, 0)),
            pl.BlockSpec((1, tq, MEM_WIDTH), lambda bb, i: (bb, i, q_col_block)),
            pl.BlockSpec((1, n_mem, MEM_WIDTH), lambda bb, i: (bb, 0, 0)),
            pl.BlockSpec((1, n_mem, MEM_WIDTH), lambda bb, i: (bb, 0, 0)),
            pl.BlockSpec((1, tq, d), lambda bb, i: (bb, i, 0)),
            pl.BlockSpec(w_mix.shape, lambda bb, i: (0, 0)),
            pl.BlockSpec(w_mem.shape, lambda bb, i: (0, 0)),
            pl.BlockSpec((1, d), lambda bb, i: (0, 0)),
            pl.BlockSpec((1, d), lambda bb, i: (0, 0)),
        ],
        out_specs=pl.BlockSpec((1, tq, d), lambda bb, i: (bb, i, 0)),
        out_shape=jax.ShapeDtypeStruct((bsz, t, d), F32),
        compiler_params=_params("arbitrary", "arbitrary"),
        name="memattn_outproj_ln",
    )(mix, qsrc, mk, mv, x, w_mix, w_mem, g, b)


def _swa_kernel(q_ref, kc_ref, vc_ref, kp_ref, vp_ref, sink_ref, o_ref, *, chunk, mask_start, hp):
    kv_dtype = F32 if hp else BF16
    i = pl.program_id(1)
    tq = q_ref.shape[1]
    q = q_ref[0]
    kall = jnp.concatenate([kp_ref[0], kc_ref[0]], axis=0)
    vall = jnp.concatenate([vp_ref[0], vc_ref[0]], axis=0)
    lane = lax.broadcasted_iota(jnp.int32, kall.shape, 1)
    lo = lane < HEAD_DIM
    kroll = pltpu.roll(kall, HEAD_DIM, 1)
    vroll = pltpu.roll(vall, HEAD_DIM, 1)
    kdup = [jnp.where(lo, kall, kroll).astype(kv_dtype), jnp.where(lo, kroll, kall).astype(kv_dtype)]
    vdup = [jnp.where(lo, vall, vroll).astype(kv_dtype), jnp.where(lo, vroll, vall).astype(kv_dtype)]
    qlane_lo = lax.broadcasted_iota(jnp.int32, (chunk, LANES), 1) < HEAD_DIM
    wlen = WINDOW + chunk
    pairs_per_group = N_PAIRS // N_KV_HEADS
    for j in range(tq // chunk):
        qj = q[j * chunk:(j + 1) * chunk]
        if mask_start:
            key_pos = i * tq + j * chunk - WINDOW + lax.broadcasted_iota(jnp.int32, (1, wlen), 1)
            key_ok = key_pos >= 0
        cols = []
        for grp in range(N_KV_HEADS):
            kb = kdup[grp][j * chunk:j * chunk + wlen]
            vb = vdup[grp][j * chunk:j * chunk + wlen]
            rows = []
            for pp in range(pairs_per_group):
                p = grp * pairs_per_group + pp
                qp = qj[:, p * LANES:(p + 1) * LANES]
                rows.append(jnp.where(qlane_lo, qp, 0.0))
                rows.append(jnp.where(qlane_lo, 0.0, qp))
            s = _dot_nt(jnp.concatenate(rows, axis=0), kb, hp) * ATT_SCALE
            if mask_start:
                s = jnp.where(key_ok, s, -jnp.inf)
            sk = sink_ref[grp][:, :1]
            m = jnp.maximum(jnp.max(s, -1, keepdims=True), sk)
            pexp = jnp.exp(s - m)
            den = jnp.sum(pexp, -1, keepdims=True) + jnp.exp(sk - m)
            o = _dot(pexp, vb, hp) / den
            for pp in range(pairs_per_group):
                o0 = o[(2 * pp) * chunk:(2 * pp + 1) * chunk]
                o1 = o[(2 * pp + 1) * chunk:(2 * pp + 2) * chunk]
                cols.append(jnp.where(qlane_lo, o0, o1))
        o_ref[0, j * chunk:(j + 1) * chunk, :] = jnp.concatenate(cols, axis=1)


def _swa(hq, kv_cur, k_col, v_col, kv_prev_k, kv_prev_v, prev_k_col, prev_v_col, prev_tiled,
         sinks, chunk, tq, hp):
    bsz, t, _ = hq.shape
    tq = min(tq, t)
    rpb = tq // WINDOW if prev_tiled else 0

    def prev_map(col):
        if prev_tiled:
            return lambda bb, i: (bb, jnp.maximum(i * rpb - 1, 0), col)
        return lambda bb, i: (bb, 0, col)

    sink_rows = jnp.broadcast_to(
        jnp.repeat(sinks.reshape(N_KV_HEADS, GQA_GROUP), chunk, axis=1)[:, :, None],
        (N_KV_HEADS, GQA_GROUP * chunk, LANES)).astype(F32)
    return pl.pallas_call(
        functools.partial(_swa_kernel, chunk=chunk, mask_start=prev_tiled, hp=hp),
        grid=(bsz, t // tq),
        in_specs=[
            pl.BlockSpec((1, tq, MIX_WIDTH), lambda bb, i: (bb, i, 0)),
            pl.BlockSpec((1, tq, LANES), lambda bb, i: (bb, i, k_col)),
            pl.BlockSpec((1, tq, LANES), lambda bb, i: (bb, i, v_col)),
            pl.BlockSpec((1, WINDOW, LANES), prev_map(prev_k_col)),
            pl.BlockSpec((1, WINDOW, LANES), prev_map(prev_v_col)),
            pl.BlockSpec(sink_rows.shape, lambda bb, i: (0, 0, 0)),
        ],
        out_specs=pl.BlockSpec((1, tq, MIX_WIDTH), lambda bb, i: (bb, i, 0)),
        out_shape=jax.ShapeDtypeStruct((bsz, t, MIX_WIDTH), F32),
        compiler_params=_params("arbitrary", "arbitrary"),
        name="swa_sink_gqa",
    )(hq, kv_cur, kv_cur, kv_prev_k, kv_prev_v, sink_rows)


def _ffn_kernel(x_ref, w1g_ref, w1u_ref, w2_ref, g_ref, b_ref, o_ref, xb_ref, acc_ref, *, hp):
    f = pl.program_id(1)

    @pl.when(f == 0)
    def _():
        xb_ref[...] = x_ref[...].astype(xb_ref.dtype)
        acc_ref[...] = jnp.zeros_like(acc_ref)

    xb = xb_ref[...]
    gate = _dot(xb, w1g_ref[...], hp)
    up = _dot(xb, w1u_ref[...], hp)
    h = gate * _sigmoid(gate) * up
    acc_ref[...] += _dot(h, w2_ref[...], hp)

    @pl.when(f == pl.num_programs(1) - 1)
    def _():
        o_ref[...] = _layer_norm(DN_ALPHA * x_ref[...] + acc_ref[...], g_ref[...], b_ref[...])


def _ffn(x, w1, w2, g, b, tm, tf, hp):
    m, d = x.shape
    f_dim = w2.shape[0]
    tm = min(tm, m)
    nf = f_dim // tf
    return pl.pallas_call(
        functools.partial(_ffn_kernel, hp=hp),
        grid=(m // tm, nf),
        in_specs=[
            pl.BlockSpec((tm, d), lambda i, f: (i, 0)),
            pl.BlockSpec((d, tf), lambda i, f: (0, f)),
            pl.BlockSpec((d, tf), lambda i, f: (0, f + nf)),
            pl.BlockSpec((tf, d), lambda i, f: (f, 0)),
            pl.BlockSpec((1, d), lambda i, f: (0, 0)),
            pl.BlockSpec((1, d), lambda i, f: (0, 0)),
        ],
        out_specs=pl.BlockSpec((tm, d), lambda i, f: (i, 0)),
        out_shape=jax.ShapeDtypeStruct((m, d), F32),
        scratch_shapes=[pltpu.VMEM((tm, d), F32 if hp else BF16), pltpu.VMEM((tm, d), F32)],
        compiler_params=_params("arbitrary", "arbitrary"),
        name="swiglu_ln",
    )(x, w1, w1, w2, g, b)


def _moe_kernel(x_ref, rt_ref, w1g_ref, w1u_ref, w2_ref, g_ref, b_ref, o_ref,
                xb_ref, acc_ref, gate_ref):
    e = pl.program_id(1)
    f = pl.program_id(2)
    first = (e == 0) & (f == 0)
    last = (e == pl.num_programs(1) - 1) & (f == pl.num_programs(2) - 1)

    @pl.when(first)
    def _():
        x = x_ref[...]
        xb_ref[...] = x.astype(BF16)
        acc_ref[...] = jnp.zeros_like(acc_ref)
        logits = _dot(x, rt_ref[...], hp=True)
        lane = lax.broadcasted_iota(jnp.int32, logits.shape, 1).astype(F32)
        lg = jnp.where(lane < N_EXPERTS, logits, -jnp.inf)
        m1 = jnp.max(lg, -1, keepdims=True)
        i1 = jnp.min(jnp.where(lg == m1, lane, float(LANES)), -1, keepdims=True)
        lg2 = jnp.where(lane == i1, -jnp.inf, lg)
        m2 = jnp.max(lg2, -1, keepdims=True)
        i2 = jnp.min(jnp.where(lg2 == m2, lane, float(LANES)), -1, keepdims=True)
        ex = jnp.exp(m2 - m1)
        g1 = 1.0 / (1.0 + ex)
        gate_ref[...] = jnp.where(lane == i1, g1, 0.0) + jnp.where(lane == i2, ex * g1, 0.0)

    xb = xb_ref[...]
    gates = gate_ref[...]
    lane = lax.broadcasted_iota(jnp.int32, gates.shape, 1)
    ge = jnp.sum(jnp.where(lane == e, gates, 0.0), -1, keepdims=True)
    gate = jnp.dot(xb, w1g_ref[0], preferred_element_type=F32)
    up = jnp.dot(xb, w1u_ref[0], preferred_element_type=F32)
    h = gate * _sigmoid(gate) * up * ge
    acc_ref[...] += jnp.dot(h.astype(BF16), w2_ref[0], preferred_element_type=F32)

    @pl.when(last)
    def _():
        o_ref[...] = _layer_norm(DN_ALPHA * x_ref[...] + acc_ref[...], g_ref[...], b_ref[...])


def _moe(x, router_pad, w1, w2, g, b, tm, tf):
    m, d = x.shape
    n_e, f_dim, _ = w2.shape
    tm = min(tm, m)
    nf = f_dim // tf
    return pl.pallas_call(
        _moe_kernel,
        grid=(m // tm, n_e, nf),
        in_specs=[
            pl.BlockSpec((tm, d), lambda i, e, f: (i, 0)),
            pl.BlockSpec((d, LANES), lambda i, e, f: (0, 0)),
            pl.BlockSpec((1, d, tf), lambda i, e, f: (e, 0, f)),
            pl.BlockSpec((1, d, tf), lambda i, e, f: (e, 0, f + nf)),
            pl.BlockSpec((1, tf, d), lambda i, e, f: (e, f, 0)),
            pl.BlockSpec((1, d), lambda i, e, f: (0, 0)),
            pl.BlockSpec((1, d), lambda i, e, f: (0, 0)),
        ],
        out_specs=pl.BlockSpec((tm, d), lambda i, e, f: (i, 0)),
        out_shape=jax.ShapeDtypeStruct((m, d), F32),
        scratch_shapes=[pltpu.VMEM((tm, d), BF16), pltpu.VMEM((tm, d), F32),
                        pltpu.VMEM((tm, LANES), F32)],
        compiler_params=_params("arbitrary", "arbitrary", "arbitrary"),
        name="moe_swiglu_ln",
    )(x, router_pad, w1, w1, w2, g, b)


def _pad_rows(x, t_to):
    t = x.shape[1]
    if t == t_to:
        return x
    return jnp.pad(x, ((0, 0), (0, t_to - t), (0, 0)))


def _head_sum_matrix():
    hid = jnp.arange(MIX_WIDTH) // HEAD_DIM
    return (hid[:, None] == hid[None, :]).astype(BF16)


def kernel(x_prompt, mem_prompt, x_sample, cache_swa_k, cache_swa_v, cache_mem_k, cache_mem_v,
           state_wkv, state_shift, w_in_a, mu_a, w0, w2_decay, a0, a2_iclr, g2_gate, k_k, k_a,
           r_k, gn_g, gn_b, w_in_b, sinks, w_kv_shared, w_mem_kv, w_out, ln1_g, ln1_b, ln2_g,
           ln2_b, ffn_w1, ffn_w2, router, moe_w1, moe_w2):
    bp, sp, d = x_prompt.shape
    bs, ts, _ = x_sample.shape
    n_mem = mem_prompt.shape[1]
    row = lambda vec: vec.reshape(1, -1).astype(F32)

    zl = jnp.zeros((DECAY_LORA, MIX_WIDTH), F32)
    wf = dict(
        in_a=w_in_a[0],
        in_b_kv=jnp.concatenate([w_in_b[0], w_kv_shared], axis=1),
        out=w_out, ffn1=ffn_w1[0], ffn2=ffn_w2[0],
        wwa=jnp.concatenate([jnp.concatenate([w2_decay[0], zl], axis=1),
                             jnp.concatenate([zl, a2_iclr[0]], axis=1)], axis=0),
        g2=g2_gate[0])
    wb = {name: w.astype(BF16) for name, w in wf.items()}
    w_memkv_b = jnp.concatenate([w_mem_kv[0], w_mem_kv[1]], axis=1).astype(BF16)
    moe_w1_b, moe_w2_b = moe_w1[0].astype(BF16), moe_w2[0].astype(BF16)
    router_pad = jnp.pad(router[0], ((0, 0), (0, LANES - N_EXPERTS)))

    def rwkv_prm(w):
        return (row(mu_a[0]), row(w0[0]), w["wwa"], row(a0[0]), w["g2"], row(k_k[0]),
                row(k_a[0]), row(r_k[0]), row(gn_g[0]), row(gn_b[0]), _head_sum_matrix())

    memkv = _mm(mem_prompt.reshape(bp * n_mem, d), w_memkv_b, 512).reshape(bp, n_mem, 4 * MEM_WIDTH)
    mem_k = [memkv[..., 0:MEM_WIDTH], memkv[..., 2 * MEM_WIDTH:3 * MEM_WIDTH]]
    mem_v = [memkv[..., MEM_WIDTH:2 * MEM_WIDTH], memkv[..., 3 * MEM_WIDTH:]]
    cmk = cache_mem_k.reshape(DEPTH, bs, n_mem, MEM_WIDTH)
    cmv = cache_mem_v.reshape(DEPTH, bs, n_mem, MEM_WIDTH)

    def layer0(x, prev0, s0, mk, mv, tm, tq, hp):
        w = wf if hp else wb
        tn = 256 if hp else None
        bsz, t, _ = x.shape
        proj = _mm(x.reshape(bsz * t, d), w["in_a"], tm, tn, hp).reshape(bsz, t, -1)
        t_pad = -(-t // WKV_CHUNK) * WKV_CHUNK
        t_valid = WKV_CHUNK if t_pad == t else t
        mix, s_bd = _rwkv(_pad_rows(proj, t_pad), prev0, _state_to_blockdiag(s0), rwkv_prm(w),
                          t_valid, hp)
        mix = mix[:, :t]
        h = _attn_out(mix, proj, A_COLS // MEM_WIDTH, mk, mv, x, w["out"][0, :MIX_WIDTH],
                      w["out"][0, MIX_WIDTH:], row(ln1_g[0]), row(ln1_b[0]), tq, hp)
        y = _ffn(h.reshape(bsz * t, d), w["ffn1"], w["ffn2"], row(ln2_g[0]), row(ln2_b[0]),
                 tm, 512, hp)
        return y.reshape(bsz, t, d), _state_from_blockdiag(s_bd)

    zero_prev = jnp.zeros((bp, 1, A_COLS), F32)
    zero_state = jnp.zeros((bp, MIX_HEADS, HEAD_DIM, HEAD_DIM), F32)
    xp, wkv_p = layer0(x_prompt, zero_prev, zero_state, mem_k[0], mem_v[0], 512, 512, False)
    prev_s = _mm(state_shift[0], wf["in_a"][:, :A_COLS], bs, 256, True).reshape(bs, 1, A_COLS)
    xs, wkv_s = layer0(x_sample, prev_s, state_wkv[0], cmk[0], cmv[0], 512, 512, True)

    def layer1(x, swa_fn, mk, mv, tm, tq, hp):
        w = wf if hp else wb
        tn = 256 if hp else None
        bsz, t, _ = x.shape
        hkv = _mm(x.reshape(bsz * t, d), w["in_b_kv"], tm, tn, hp).reshape(bsz, t, -1)
        mix = swa_fn(hkv)
        h = _attn_out(mix, hkv, MIX_WIDTH // MEM_WIDTH, mk, mv, x, w["out"][1, :MIX_WIDTH],
                      w["out"][1, MIX_WIDTH:], row(ln1_g[1]), row(ln1_b[1]), tq, hp)
        y = _moe(h.reshape(bsz * t, d), router_pad, moe_w1_b, moe_w2_b, row(ln2_g[1]),
                 row(ln2_b[1]), tm, 512)
        return y.reshape(bsz, t, d), hkv

    k_col = d // LANES
    v_col = k_col + 1
    swa_p = lambda hkv: _swa(hkv, hkv, k_col, v_col, hkv, hkv, k_col, v_col, True, sinks[0],
                             64, 256, False)
    ck = cache_swa_k.reshape(bs, WINDOW, LANES)
    cv = cache_swa_v.reshape(bs, WINDOW, LANES)
    swa_s = lambda hkv: _swa(hkv, hkv, k_col, v_col, ck, cv, 0, 0, False, sinks[0], ts, ts, True)
    yp, hkv_p = layer1(xp, swa_p, mem_k[1], mem_v[1], 512, 512, False)
    ys, hkv_s = layer1(xs, swa_s, cmk[1], cmv[1], 512, 512, True)

    kv_heads = lambda a: a.reshape(a.shape[0], a.shape[1], N_KV_HEADS, HEAD_DIM)
    mem_heads = lambda a: a.reshape(bp, n_mem, MEM_HEADS, HEAD_DIM)
    return (yp, ys,
            kv_heads(hkv_p[:, -WINDOW:, d:d + LANES]), kv_heads(hkv_p[:, -WINDOW:, d + LANES:]),
            kv_heads(hkv_s[:, :, d:d + LANES]), kv_heads(hkv_s[:, :, d + LANES:]),
            jnp.stack([mem_heads(mem_k[0]), mem_heads(mem_k[1])]),
            jnp.stack([mem_heads(mem_v[0]), mem_heads(mem_v[1])]),
            wkv_p[None], x_prompt[:, -1][None], wkv_s[None], x_sample[:, -1][None])
```

```python
import functools

import jax
import jax.numpy as jnp
from jax import lax
from jax.experimental import pallas as pl
from jax.experimental.pallas import tpu as pltpu

F32 = jnp.float32
BF16 = jnp.bfloat16

HEAD_DIM = 64
LANES = 128
MIX_HEADS = 12
MIX_WIDTH = MIX_HEADS * HEAD_DIM
N_PAIRS = MIX_HEADS // 2
MEM_HEADS = 4
MEM_WIDTH = MEM_HEADS * HEAD_DIM
DECAY_LORA = 64
AAA_LORA = 64
GATE_LORA = 128
A_COLS = 3 * MIX_WIDTH + DECAY_LORA + AAA_LORA + GATE_LORA
GN_EPS = 64e-5
WINDOW = 128
N_KV_HEADS = 2
GQA_GROUP = MIX_HEADS // N_KV_HEADS
ATT_SCALE = HEAD_DIM ** -0.5
N_EXPERTS = 8
DEPTH = 2
DN_ALPHA = (2 * DEPTH) ** 0.25
LN_EPS = 1e-5
WKV_CHUNK = 64
VMEM_LIMIT = 56 * 1024 * 1024


def _params(*sem):
    return pltpu.CompilerParams(dimension_semantics=sem, vmem_limit_bytes=VMEM_LIMIT)


def _dot(a, b, hp=False):
    if hp:
        return jnp.dot(a.astype(F32), b.astype(F32), preferred_element_type=F32,
                       precision=lax.Precision.HIGHEST)
    return jnp.dot(a.astype(BF16), b.astype(BF16), preferred_element_type=F32)


def _dot_nt(a, b, hp=False):
    dims = (((1,), (1,)), ((), ()))
    if hp:
        return lax.dot_general(a.astype(F32), b.astype(F32), dims, preferred_element_type=F32,
                               precision=lax.Precision.HIGHEST)
    return lax.dot_general(a.astype(BF16), b.astype(BF16), dims, preferred_element_type=F32)


def _split3(a):
    hi = a.astype(BF16)
    r1 = a - hi.astype(F32)
    mid = r1.astype(BF16)
    lo = (r1 - mid.astype(F32)).astype(BF16)
    return hi, mid, lo


def _sigmoid(x):
    return 1.0 / (1.0 + jnp.exp(-x))


def _layer_norm(v, g, b):
    mu = jnp.mean(v, -1, keepdims=True)
    d = v - mu
    var = jnp.mean(d * d, -1, keepdims=True)
    return d * lax.rsqrt(var + LN_EPS) * g + b


def _mm_kernel(x_ref, w_ref, o_ref, *, hp):
    o_ref[...] = _dot(x_ref[...], w_ref[...], hp)


def _mm(x, w, tm, tn=None, hp=False):
    m, k = x.shape
    n = w.shape[1]
    tm = min(tm, m)
    tn = n if tn is None else tn
    return pl.pallas_call(
        functools.partial(_mm_kernel, hp=hp),
        grid=(m // tm, n // tn),
        in_specs=[pl.BlockSpec((tm, k), lambda i, j: (i, 0)),
                  pl.BlockSpec((k, tn), lambda i, j: (0, j))],
        out_specs=pl.BlockSpec((tm, tn), lambda i, j: (i, j)),
        out_shape=jax.ShapeDtypeStruct((m, n), F32),
        compiler_params=_params("arbitrary", "arbitrary"),
        name="proj_mm",
    )(x, w)


def _stack_masked(x, lane_lo):
    return jnp.concatenate([jnp.where(lane_lo, x, 0.0), jnp.where(lane_lo, 0.0, x)], axis=0)


def _rwkv_kernel(proj_ref, prev0_ref, s0_ref, mu_ref, w0_ref, wwa_ref, a0_ref, g2_ref,
                 kk_ref, ka_ref, rk_ref, gng_ref, gnb_ref,
                 mix_ref, sfin_ref, s_ref, prev_ref, *, t_valid, hp):
    dot = functools.partial(_dot, hp=hp)
    dot_nt = functools.partial(_dot_nt, hp=hp)
    c = pl.program_id(1)
    n_c = pl.num_programs(1)
    C = proj_ref.shape[1]

    @pl.when(c == 0)
    def _():
        s_ref[...] = s0_ref[0]
        prev_ref[...] = prev0_ref[0]

    pa = proj_ref[0][:, :A_COLS]
    row = lax.broadcasted_iota(jnp.int32, (C, 1), 0)
    pa_prev = jnp.where(row == 0, prev_ref[...], pltpu.roll(pa, 1, 0))
    prev_ref[...] = pa[C - 1:C, :]
    z = pa + (pa_prev - pa) * mu_ref[...]
    r = z[:, :MIX_WIDTH]
    k = z[:, MIX_WIDTH:2 * MIX_WIDTH]
    v = z[:, 2 * MIX_WIDTH:3 * MIX_WIDTH]
    zwa = z[:, 3 * MIX_WIDTH:3 * MIX_WIDTH + LANES]
    zg = z[:, 3 * MIX_WIDTH + LANES:A_COLS]
    lane = lax.broadcasted_iota(jnp.int32, (C, LANES), 1)
    lane_lo = lane < HEAD_DIM
    wa = dot(jnp.where(lane_lo, jnp.tanh(zwa), zwa), wwa_ref[...])
    wl = w0_ref[...] + wa[:, :MIX_WIDTH]
    softplus = jnp.maximum(-wl, 0.0) + jnp.log(1.0 + jnp.exp(-jnp.abs(wl)))
    lw = -jnp.exp(-softplus - 0.5)
    a = _sigmoid(a0_ref[...] + wa[:, MIX_WIDTH:])
    g = dot(_sigmoid(zg), g2_ref[...])
    kk = k * kk_ref[...]
    k2 = k * (1.0 + (a - 1.0) * ka_ref[...])
    rk = r * k2 * rk_ref[...]
    bd_r = lax.broadcasted_iota(jnp.int32, (LANES, LANES), 0) < HEAD_DIM
    bd_c = lax.broadcasted_iota(jnp.int32, (LANES, LANES), 1) < HEAD_DIM
    bd_mask = bd_r == bd_c
    head_ones = jnp.where(bd_mask, 1.0, 0.0).astype(BF16)

    def head_sums(xs):
        rows = [x[:, p * LANES:(p + 1) * LANES] for x in xs for p in range(N_PAIRS)]
        s = _dot_x3_rhs_exact(jnp.concatenate(rows, axis=0), head_ones)
        return [jnp.concatenate([s[(i * N_PAIRS + p) * C:(i * N_PAIRS + p + 1) * C]
                                 for p in range(N_PAIRS)], axis=1) for i in range(len(xs))]

    kk_sq, rk_sum = head_sums([kk * kk, rk])
    kk = kk * lax.rsqrt(jnp.maximum(kk_sq, 1e-24))
    bonus = rk_sum * v
    if t_valid < C:
        live = row < t_valid
        lw = jnp.where(live, lw, 0.0)
        kk = jnp.where(live, kk, 0.0)
        k2 = jnp.where(live, k2, 0.0)

    ri = lax.broadcasted_iota(jnp.int32, (C, C), 0)
    ci = lax.broadcasted_iota(jnp.int32, (C, C), 1)
    tri = jnp.where(ri >= ci, 1.0, 0.0).astype(BF16)
    h3, m3, l3 = _split3(lw)
    d = functools.partial(jnp.dot, preferred_element_type=F32)
    cum = d(tri, h3) + (d(tri, m3) + d(tri, l3))
    cum_last = cum[C - 1:C, :]
    e_inc = jnp.exp(cum)
    e_ninc = jnp.exp(-cum)
    kka = kk * a
    at = -kk * jnp.exp(cum - lw)
    rt = r * e_inc
    bt = kka * e_ninc
    kt = k2 * e_ninc
    e_rem = jnp.exp(cum_last - cum)
    bg = kka * e_rem
    kg = k2 * e_rem
    gam = jnp.exp(cum_last)

    C2 = 2 * C
    r2 = lax.broadcasted_iota(jnp.int32, (C2, C2), 0)
    c2 = lax.broadcasted_iota(jnp.int32, (C2, C2), 1)
    strict = r2 > c2
    incl = r2 >= c2
    wc = lax.broadcasted_iota(jnp.int32, (C, C2), 1)
    w_lo = wc < C

    def wide(x):
        return x[:C] + x[C:]

    def blockdiag(x):
        return jnp.concatenate([jnp.where(w_lo, x, 0.0), jnp.where(w_lo, 0.0, x)], axis=0)

    pairs = range(N_PAIRS)
    sls = [slice(p * LANES, (p + 1) * LANES) for p in pairs]
    sm = lambda x: _stack_masked(x, lane_lo)
    gms = [dot_nt(jnp.concatenate([sm(at[:, sl]), sm(rt[:, sl])], axis=0),
                  jnp.concatenate([sm(bt[:, sl]), sm(kt[:, sl])], axis=0)) for sl in sls]
    pw = [wide(jnp.where(strict, gm[:C2, :C2], 0.0)) for gm in gms]
    l_ak = [wide(jnp.where(strict, gm[:C2, C2:], 0.0)) for gm in gms]
    a_r = [jnp.concatenate([wide(jnp.where(incl, gm[C2:, :C2], 0.0)),
                            wide(jnp.where(incl, gm[C2:, C2:], 0.0))], axis=1) for gm in gms]
    s_old = [s_ref[p] for p in pairs]
    a_s = [dot_nt(jnp.concatenate([at[:, sl], rt[:, sl]], axis=0), s_old[p])
           for p, sl in enumerate(sls)]
    v_sm = [sm(v[:, sl]) for sl in sls]
    x = [a_s[p][:C] + dot(l_ak[p], v_sm[p]) for p in pairs]
    n_inv = pw
    span = 1
    while span * 2 < C:
        pw = [dot(m, blockdiag(m)) for m in pw]
        n_inv = [n_inv[p] + pw[p] + dot(n_inv[p], blockdiag(pw[p])) for p in pairs]
        span *= 2
    u = [x[p] + dot(n_inv[p], sm(x[p])) for p in pairs]
    outs = [a_s[p][C:] + dot(a_r[p], jnp.concatenate([sm(u[p]), v_sm[p]], axis=0)) for p in pairs]
    for p, sl in enumerate(sls):
        uv_t = jnp.concatenate([u[p], v[:, sl]], axis=0).T
        upd = dot(uv_t, jnp.concatenate([bg[:, sl], kg[:, sl]], axis=0))
        s_ref[p] = s_old[p] * gam[:, sl] + jnp.where(bd_mask, upd, 0.0)
    y = jnp.concatenate(outs, axis=1)

    inv_n = 1.0 / HEAD_DIM
    ym = head_sums([y])[0] * inv_n
    yd = y - ym
    yv = head_sums([yd * yd])[0] * inv_n
    yn = yd * lax.rsqrt(yv + GN_EPS) * gng_ref[...] + gnb_ref[...]
    mix_ref[0] = (yn + bonus) * g

    @pl.when(c == n_c - 1)
    def _():
        sfin_ref[0] = s_ref[...]


def _dot_x3_rhs_exact(a, b_exact):
    h, m, l = _split3(a)
    d = functools.partial(jnp.dot, preferred_element_type=F32)
    return d(h, b_exact) + (d(m, b_exact) + d(l, b_exact))


def _rwkv(proj, prev0, s0_bd, prm, t_valid, hp):
    bsz, t, npj = proj.shape
    C = WKV_CHUNK
    full = lambda shp: pl.BlockSpec(shp, lambda b, c: (0,) * len(shp))
    in_specs = [
        pl.BlockSpec((1, C, npj), lambda b, c: (b, c, 0)),
        pl.BlockSpec((1, 1, A_COLS), lambda b, c: (b, 0, 0)),
        pl.BlockSpec((1, N_PAIRS, LANES, LANES), lambda b, c: (b, 0, 0, 0)),
    ] + [full(x.shape) for x in prm]
    return pl.pallas_call(
        functools.partial(_rwkv_kernel, t_valid=t_valid, hp=hp),
        grid=(bsz, t // C),
        in_specs=in_specs,
        out_specs=[pl.BlockSpec((1, C, MIX_WIDTH), lambda b, c: (b, c, 0)),
                   pl.BlockSpec((1, N_PAIRS, LANES, LANES), lambda b, c: (b, 0, 0, 0))],
        out_shape=[jax.ShapeDtypeStruct((bsz, t, MIX_WIDTH), F32),
                   jax.ShapeDtypeStruct((bsz, N_PAIRS, LANES, LANES), F32)],
        scratch_shapes=[pltpu.VMEM((N_PAIRS, LANES, LANES), F32),
                        pltpu.VMEM((1, A_COLS), F32)],
        compiler_params=_params("arbitrary", "arbitrary"),
        name="rwkv7_chunked",
    )(proj, prev0, s0_bd, *prm)


def _state_to_blockdiag(s):
    bsz = s.shape[0]
    sp = s.reshape(bsz, N_PAIRS, 2, HEAD_DIM, HEAD_DIM)
    z = jnp.zeros_like(sp[:, :, 0])
    top = jnp.concatenate([sp[:, :, 0], z], axis=-1)
    bot = jnp.concatenate([z, sp[:, :, 1]], axis=-1)
    return jnp.concatenate([top, bot], axis=-2)


def _state_from_blockdiag(s_bd):
    bsz = s_bd.shape[0]
    h0 = s_bd[:, :, :HEAD_DIM, :HEAD_DIM]
    h1 = s_bd[:, :, HEAD_DIM:, HEAD_DIM:]
    return jnp.stack([h0, h1], axis=2).reshape(bsz, MIX_HEADS, HEAD_DIM, HEAD_DIM)


def _attn_out_kernel(mix_ref, qm_ref, mk_ref, mv_ref, x_ref, wmix_ref, wmem_ref, g_ref, b_ref, o_ref,
                     *, hp):
    qm = qm_ref[0]
    mk = mk_ref[0] if hp else mk_ref[0].astype(BF16)
    mv = mv_ref[0] if hp else mv_ref[0].astype(BF16)
    lane = lax.broadcasted_iota(jnp.int32, qm.shape, 1)
    mem = jnp.zeros_like(qm)
    for h in range(MEM_HEADS):
        in_h = (lane >= h * HEAD_DIM) & (lane < (h + 1) * HEAD_DIM)
        s = _dot_nt(jnp.where(in_h, qm, 0.0), mk, hp) * ATT_SCALE
        p = jnp.exp(s - jnp.max(s, -1, keepdims=True))
        den = jnp.sum(p, -1, keepdims=True)
        mem = jnp.where(in_h, _dot(p, mv, hp) / den, mem)
    y = _dot(mix_ref[0], wmix_ref[...], hp) + _dot(mem, wmem_ref[...], hp)
    o_ref[0] = _layer_norm(DN_ALPHA * x_ref[0] + y, g_ref[...], b_ref[...])


def _attn_out(mix, qsrc, q_col_block, mk, mv, x, w_mix, w_mem, g, b, tq, hp):
    bsz, t, d = x.shape
    tq = min(tq, t)
    n_mem = mk.shape[1]
    return pl.pallas_call(
        functools.partial(_attn_out_kernel, hp=hp),
        grid=(bsz, t // tq),
        in_specs=[
            pl.BlockSpec((1, tq, MIX_WIDTH), lambda bb, i: (bb, i, 0)),
            pl.BlockSpec((1, tq, MEM_WIDTH), lambda bb, i: (bb, i, q_col_block)),
            pl.BlockSpec((1, n_mem, MEM_WIDTH), lambda bb, i: (bb, 0, 0)),
            pl.BlockSpec((1, n_mem, MEM_WIDTH), lambda bb, i: (bb, 0, 0)),
            pl.BlockSpec((1, tq, d), lambda bb, i: (bb, i, 0)),
            pl.BlockSpec(w_mix.shape, lambda bb, i: (0, 0)),
            pl.BlockSpec(w_mem.shape, lambda bb, i: (0, 0)),
            pl.BlockSpec((1, d), lambda bb, i: (0, 0)),
            pl.BlockSpec((1, d), lambda bb, i: (0, 0)),
        ],
        out_specs=pl.BlockSpec((1, tq, d), lambda bb, i: (bb, i, 0)),
        out_shape=jax.ShapeDtypeStruct((bsz, t, d), F32),
        compiler_params=_params("arbitrary", "arbitrary"),
        name="memattn_outproj_ln",
    )(mix, qsrc, mk, mv, x, w_mix, w_mem, g, b)


def _swa_kernel(q_ref, kc_ref, vc_ref, kp_ref, vp_ref, sink_ref, o_ref, *, chunk, mask_start, hp):
    kv_dtype = F32 if hp else BF16
    i = pl.program_id(1)
    tq = q_ref.shape[1]
    q = q_ref[0]
    kall = jnp.concatenate([kp_ref[0], kc_ref[0]], axis=0)
    vall = jnp.concatenate([vp_ref[0], vc_ref[0]], axis=0)
    lane = lax.broadcasted_iota(jnp.int32, kall.shape, 1)
    lo = lane < HEAD_DIM
    kroll = pltpu.roll(kall, HEAD_DIM, 1)
    vroll = pltpu.roll(vall, HEAD_DIM, 1)
    kdup = [jnp.where(lo, kall, kroll).astype(kv_dtype), jnp.where(lo, kroll, kall).astype(kv_dtype)]
    vdup = [jnp.where(lo, vall, vroll).astype(kv_dtype), jnp.where(lo, vroll, vall).astype(kv_dtype)]
    qlane_lo = lax.broadcasted_iota(jnp.int32, (chunk, LANES), 1) < HEAD_DIM
    wlen = WINDOW + chunk
    pairs_per_group = N_PAIRS // N_KV_HEADS
    for j in range(tq // chunk):
        qj = q[j * chunk:(j + 1) * chunk]
        if mask_start:
            key_pos = i * tq + j * chunk - WINDOW + lax.broadcasted_iota(jnp.int32, (1, wlen), 1)
            key_ok = key_pos >= 0
        cols = []
        for grp in range(N_KV_HEADS):
            kb = kdup[grp][j * chunk:j * chunk + wlen]
            vb = vdup[grp][j * chunk:j * chunk + wlen]
            rows = []
            for pp in range(pairs_per_group):
                p = grp * pairs_per_group + pp
                qp = qj[:, p * LANES:(p + 1) * LANES]
                rows.append(jnp.where(qlane_lo, qp, 0.0))
                rows.append(jnp.where(qlane_lo, 0.0, qp))
            s = _dot_nt(jnp.concatenate(rows, axis=0), kb, hp) * ATT_SCALE
            if mask_start:
                s = jnp.where(key_ok, s, -jnp.inf)
            sk = sink_ref[grp][:, :1]
            m = jnp.maximum(jnp.max(s, -1, keepdims=True), sk)
            pexp = jnp.exp(s - m)
            den = jnp.sum(pexp, -1, keepdims=True) + jnp.exp(sk - m)
            o = _dot(pexp, vb, hp) / den
            for pp in range(pairs_per_group):
                o0 = o[(2 * pp) * chunk:(2 * pp + 1) * chunk]
                o1 = o[(2 * pp + 1) * chunk:(2 * pp + 2) * chunk]
                cols.append(jnp.where(qlane_lo, o0, o1))
        o_ref[0, j * chunk:(j + 1) * chunk, :] = jnp.concatenate(cols, axis=1)


def _swa(hq, kv_cur, k_col, v_col, kv_prev_k, kv_prev_v, prev_k_col, prev_v_col, prev_tiled,
         sinks, chunk, tq, hp):
    bsz, t, _ = hq.shape
    tq = min(tq, t)
    rpb = tq // WINDOW if prev_tiled else 0

    def prev_map(col):
        if prev_tiled:
            return lambda bb, i: (bb, jnp.maximum(i * rpb - 1, 0), col)
        return lambda bb, i: (bb, 0, col)

    sink_rows = jnp.broadcast_to(
        jnp.repeat(sinks.reshape(N_KV_HEADS, GQA_GROUP), chunk, axis=1)[:, :, None],
        (N_KV_HEADS, GQA_GROUP * chunk, LANES)).astype(F32)
    return pl.pallas_call(
        functools.partial(_swa_kernel, chunk=chunk, mask_start=prev_tiled, hp=hp),
        grid=(bsz, t // tq),
        in_specs=[
            pl.BlockSpec((1, tq, MIX_WIDTH), lambda bb, i: (bb, i, 0)),
            pl.BlockSpec((1, tq, LANES), lambda bb, i: (bb, i, k_col)),
            pl.BlockSpec((1, tq, LANES), lambda bb, i: (bb, i, v_col)),
            pl.BlockSpec((1, WINDOW, LANES), prev_map(prev_k_col)),
            pl.BlockSpec((1, WINDOW, LANES), prev_map(prev_v_col)),
            pl.BlockSpec(sink_rows.shape, lambda bb, i: (0, 0, 0)),
        ],
        out_specs=pl.BlockSpec((1, tq, MIX_WIDTH), lambda bb, i: (bb, i, 0)),
        out_shape=jax.ShapeDtypeStruct((bsz, t, MIX_WIDTH), F32),
        compiler_params=_params("arbitrary", "arbitrary"),
        name="swa_sink_gqa",
    )(hq, kv_cur, kv_cur, kv_prev_k, kv_prev_v, sink_rows)


def _ffn_kernel(x_ref, w1g_ref, w1u_ref, w2_ref, g_ref, b_ref, o_ref, xb_ref, acc_ref, *, hp):
    f = pl.program_id(1)

    @pl.when(f == 0)
    def _():
        xb_ref[...] = x_ref[...].astype(xb_ref.dtype)
        acc_ref[...] = jnp.zeros_like(acc_ref)

    xb = xb_ref[...]
    gate = _dot(xb, w1g_ref[...], hp)
    up = _dot(xb, w1u_ref[...], hp)
    h = gate * _sigmoid(gate) * up
    acc_ref[...] += _dot(h, w2_ref[...], hp)

    @pl.when(f == pl.num_programs(1) - 1)
    def _():
        o_ref[...] = _layer_norm(DN_ALPHA * x_ref[...] + acc_ref[...], g_ref[...], b_ref[...])


def _ffn(x, w1, w2, g, b, tm, tf, hp):
    m, d = x.shape
    f_dim = w2.shape[0]
    tm = min(tm, m)
    nf = f_dim // tf
    return pl.pallas_call(
        functools.partial(_ffn_kernel, hp=hp),
        grid=(m // tm, nf),
        in_specs=[
            pl.BlockSpec((tm, d), lambda i, f: (i, 0)),
            pl.BlockSpec((d, tf), lambda i, f: (0, f)),
            pl.BlockSpec((d, tf), lambda i, f: (0, f + nf)),
            pl.BlockSpec((tf, d), lambda i, f: (f, 0)),
            pl.BlockSpec((1, d), lambda i, f: (0, 0)),
            pl.BlockSpec((1, d), lambda i, f: (0, 0)),
        ],
        out_specs=pl.BlockSpec((tm, d), lambda i, f: (i, 0)),
        out_shape=jax.ShapeDtypeStruct((m, d), F32),
        scratch_shapes=[pltpu.VMEM((tm, d), F32 if hp else BF16), pltpu.VMEM((tm, d), F32)],
        compiler_params=_params("arbitrary", "arbitrary"),
        name="swiglu_ln",
    )(x, w1, w1, w2, g, b)


def _moe_kernel(x_ref, rt_ref, w1g_ref, w1u_ref, w2_ref, g_ref, b_ref, o_ref,
                xb_ref, acc_ref, gate_ref):
    e = pl.program_id(1)
    f = pl.program_id(2)
    first = (e == 0) & (f == 0)
    last = (e == pl.num_programs(1) - 1) & (f == pl.num_programs(2) - 1)

    @pl.when(first)
    def _():
        x = x_ref[...]
        xb_ref[...] = x.astype(BF16)
        acc_ref[...] = jnp.zeros_like(acc_ref)
        logits = _dot(x, rt_ref[...], hp=True)
        lane = lax.broadcasted_iota(jnp.int32, logits.shape, 1).astype(F32)
        lg = jnp.where(lane < N_EXPERTS, logits, -jnp.inf)
        m1 = jnp.max(lg, -1, keepdims=True)
        i1 = jnp.min(jnp.where(lg == m1, lane, float(LANES)), -1, keepdims=True)
        lg2 = jnp.where(lane == i1, -jnp.inf, lg)
        m2 = jnp.max(lg2, -1, keepdims=True)
        i2 = jnp.min(jnp.where(lg2 == m2, lane, float(LANES)), -1, keepdims=True)
        ex = jnp.exp(m2 - m1)
        g1 = 1.0 / (1.0 + ex)
        gate_ref[...] = jnp.where(lane == i1, g1, 0.0) + jnp.where(lane == i2, ex * g1, 0.0)

    xb = xb_ref[...]
    gates = gate_ref[...]
    lane = lax.broadcasted_iota(jnp.int32, gates.shape, 1)
    ge = jnp.sum(jnp.where(lane == e, gates, 0.0), -1, keepdims=True)
    gate = jnp.dot(xb, w1g_ref[0], preferred_element_type=F32)
    up = jnp.dot(xb, w1u_ref[0], preferred_element_type=F32)
    h = gate * _sigmoid(gate) * up * ge
    acc_ref[...] += jnp.dot(h.astype(BF16), w2_ref[0], preferred_element_type=F32)

    @pl.when(last)
    def _():
        o_ref[...] = _layer_norm(DN_ALPHA * x_ref[...] + acc_ref[...], g_ref[...], b_ref[...])


def _moe(x, router_pad, w1, w2, g, b, tm, tf):
    m, d = x.shape
    n_e, f_dim, _ = w2.shape
    tm = min(tm, m)
    nf = f_dim // tf
    return pl.pallas_call(
        _moe_kernel,
        grid=(m // tm, n_e, nf),
        in_specs=[
            pl.BlockSpec((tm, d), lambda i, e, f: (i, 0)),
            pl.BlockSpec((d, LANES), lambda i, e, f: (0, 0)),
            pl.BlockSpec((1, d, tf), lambda i, e, f: (e, 0, f)),
            pl.BlockSpec((1, d, tf), lambda i, e, f: (e, 0, f + nf)),
            pl.BlockSpec((1, tf, d), lambda i, e, f: (e, f, 0)),
            pl.BlockSpec((1, d), lambda i, e, f: (0, 0)),
            pl.BlockSpec((1, d), lambda i, e, f: (0, 0)),
        ],
        out_specs=pl.BlockSpec((tm, d), lambda i, e, f: (i, 0)),
        out_shape=jax.ShapeDtypeStruct((m, d), F32),
        scratch_shapes=[pltpu.VMEM((tm, d), BF16), pltpu.VMEM((tm, d), F32),
                        pltpu.VMEM((tm, LANES), F32)],
        compiler_params=_params("arbitrary", "arbitrary", "arbitrary"),
        name="moe_swiglu_ln",
    )(x, router_pad, w1, w1, w2, g, b)


def _pad_rows(x, t_to):
    t = x.shape[1]
    if t == t_to:
        return x
    return jnp.pad(x, ((0, 0), (0, t_to - t), (0, 0)))


def kernel(x_prompt, mem_prompt, x_sample, cache_swa_k, cache_swa_v, cache_mem_k, cache_mem_v,
           state_wkv, state_shift, w_in_a, mu_a, w0, w2_decay, a0, a2_iclr, g2_gate, k_k, k_a,
           r_k, gn_g, gn_b, w_in_b, sinks, w_kv_shared, w_mem_kv, w_out, ln1_g, ln1_b, ln2_g,
           ln2_b, ffn_w1, ffn_w2, router, moe_w1, moe_w2):
    bp, sp, d = x_prompt.shape
    bs, ts, _ = x_sample.shape
    n_mem = mem_prompt.shape[1]
    row = lambda vec: vec.reshape(1, -1).astype(F32)

    zl = jnp.zeros((DECAY_LORA, MIX_WIDTH), F32)
    wf = dict(
        in_a=w_in_a[0],
        in_b_kv=jnp.concatenate([w_in_b[0], w_kv_shared], axis=1),
        out=w_out, ffn1=ffn_w1[0], ffn2=ffn_w2[0],
        wwa=jnp.concatenate([jnp.concatenate([w2_decay[0], zl], axis=1),
                             jnp.concatenate([zl, a2_iclr[0]], axis=1)], axis=0),
        g2=g2_gate[0])
    wb = {name: w.astype(BF16) for name, w in wf.items()}
    w_memkv_b = jnp.concatenate([w_mem_kv[0], w_mem_kv[1]], axis=1).astype(BF16)
    moe_w1_b, moe_w2_b = moe_w1[0].astype(BF16), moe_w2[0].astype(BF16)
    router_pad = jnp.pad(router[0], ((0, 0), (0, LANES - N_EXPERTS)))

    def rwkv_prm(w):
        return (row(mu_a[0]), row(w0[0]), w["wwa"], row(a0[0]), w["g2"], row(k_k[0]),
                row(k_a[0]), row(r_k[0]), row(gn_g[0]), row(gn_b[0]))

    memkv = _mm(mem_prompt.reshape(bp * n_mem, d), w_memkv_b, 512).reshape(bp, n_mem, 4 * MEM_WIDTH)
    mem_k = [memkv[..., 0:MEM_WIDTH], memkv[..., 2 * MEM_WIDTH:3 * MEM_WIDTH]]
    mem_v = [memkv[..., MEM_WIDTH:2 * MEM_WIDTH], memkv[..., 3 * MEM_WIDTH:]]
    cmk = cache_mem_k.reshape(DEPTH, bs, n_mem, MEM_WIDTH)
    cmv = cache_mem_v.reshape(DEPTH, bs, n_mem, MEM_WIDTH)

    def layer0(x, prev0, s0, mk, mv, tm, tq, hp):
        w = wf if hp else wb
        tn = 256 if hp else None
        bsz, t, _ = x.shape
        proj = _mm(x.reshape(bsz * t, d), w["in_a"], tm, tn, hp).reshape(bsz, t, -1)
        t_pad = -(-t // WKV_CHUNK) * WKV_CHUNK
        t_valid = WKV_CHUNK if t_pad == t else t
        mix, s_bd = _rwkv(_pad_rows(proj, t_pad), prev0, _state_to_blockdiag(s0), rwkv_prm(w),
                          t_valid, hp)
        mix = mix[:, :t]
        h = _attn_out(mix, proj, A_COLS // MEM_WIDTH, mk, mv, x, w["out"][0, :MIX_WIDTH],
                      w["out"][0, MIX_WIDTH:], row(ln1_g[0]), row(ln1_b[0]), tq, hp)
        y = _ffn(h.reshape(bsz * t, d), w["ffn1"], w["ffn2"], row(ln2_g[0]), row(ln2_b[0]),
                 tm, 512, hp)
        return y.reshape(bsz, t, d), _state_from_blockdiag(s_bd)

    zero_prev = jnp.zeros((bp, 1, A_COLS), F32)
    zero_state = jnp.zeros((bp, MIX_HEADS, HEAD_DIM, HEAD_DIM), F32)
    xp, wkv_p = layer0(x_prompt, zero_prev, zero_state, mem_k[0], mem_v[0], 512, 512, False)
    prev_s = _mm(state_shift[0], wf["in_a"][:, :A_COLS], bs, 256, True).reshape(bs, 1, A_COLS)
    xs, wkv_s = layer0(x_sample, prev_s, state_wkv[0], cmk[0], cmv[0], 512, 512, True)

    def layer1(x, swa_fn, mk, mv, tm, tq, hp):
        w = wf if hp else wb
        tn = 256 if hp else None
        bsz, t, _ = x.shape
        hkv = _mm(x.reshape(bsz * t, d), w["in_b_kv"], tm, tn, hp).reshape(bsz, t, -1)
        mix = swa_fn(hkv)
        h = _attn_out(mix, hkv, MIX_WIDTH // MEM_WIDTH, mk, mv, x, w["out"][1, :MIX_WIDTH],
                      w["out"][1, MIX_WIDTH:], row(ln1_g[1]), row(ln1_b[1]), tq, hp)
        y = _moe(h.reshape(bsz * t, d), router_pad, moe_w1_b, moe_w2_b, row(ln2_g[1]),
                 row(ln2_b[1]), tm, 512)
        return y.reshape(bsz, t, d), hkv

    k_col = d // LANES
    v_col = k_col + 1
    swa_p = lambda hkv: _swa(hkv, hkv, k_col, v_col, hkv, hkv, k_col, v_col, True, sinks[0],
                             64, 256, False)
    ck = cache_swa_k.reshape(bs, WINDOW, LANES)
    cv = cache_swa_v.reshape(bs, WINDOW, LANES)
    swa_s = lambda hkv: _swa(hkv, hkv, k_col, v_col, ck, cv, 0, 0, False, sinks[0], ts, ts, True)
    yp, hkv_p = layer1(xp, swa_p, mem_k[1], mem_v[1], 512, 512, False)
    ys, hkv_s = layer1(xs, swa_s, cmk[1], cmv[1], 512, 512, True)

    kv_heads = lambda a: a.reshape(a.shape[0], a.shape[1], N_KV_HEADS, HEAD_DIM)
    mem_heads = lambda a: a.reshape(bp, n_mem, MEM_HEADS, HEAD_DIM)
    return (yp, ys,
            kv_heads(hkv_p[:, -WINDOW:, d:d + LANES]), kv_heads(hkv_p[:, -WINDOW:, d + LANES:]),
            kv_heads(hkv_s[:, :, d:d + LANES]), kv_heads(hkv_s[:, :, d + LANES:]),
            jnp.stack([mem_heads(mem_k[0]), mem_heads(mem_k[1])]),
            jnp.stack([mem_heads(mem_v[0]), mem_heads(mem_v[1])]),
            wkv_p[None], x_prompt[:, -1][None], wkv_s[None], x_sample[:, -1][None])
```
